```python
import math
import jax, jax.numpy as jnp
from jax import lax
import numpy as np

D_MODEL = 1024
BATCH = 8
SEQ = 4096
DEPTH = 2

CHUNK = 64
Q_BLOCK = 128
ROPE_THETA = 500000.0
NORM_EPS = 1e-6

A_HEADS = 8
A_NOPE = 64
A_ROPE = 32
A_V = 64
A_Q_LORA = 256
A_KV_LORA = 128
B_HEADS = 8
B_DIM = 32
B_V = 2 * B_DIM
B_ROPE = B_DIM // 4
AB_SPLITS = [A_Q_LORA,
             A_Q_LORA + A_KV_LORA,
             A_Q_LORA + A_KV_LORA + A_ROPE,
             A_Q_LORA + A_KV_LORA + A_ROPE + B_HEADS * 2 * B_DIM,
             A_Q_LORA + A_KV_LORA + A_ROPE + 2 * B_HEADS * 2 * B_DIM]
AB_IN = A_Q_LORA + A_KV_LORA + A_ROPE + 2 * B_HEADS * 2 * B_DIM + B_HEADS * B_V
AB_OUT = A_HEADS * A_V + B_HEADS * B_V
C_HEADS = 16
C_DIM = D_MODEL // C_HEADS
C_PAST_CHUNKS = 8
C_PAST = C_PAST_CHUNKS * CHUNK
C_BAND = C_PAST + CHUNK
REL_CLIP = 128
PEER_HEADS = 8
N_KEYS = 128
N_EXPERTS = N_KEYS * N_KEYS
PEER_TOPK = 16
PEER_DQ = 256
PEER_TOKEN_BLOCK = 128

kernel_name = "hybrid_mla_diff_chunkbias_peer"


def rms_norm(x, g):
    x32 = x.astype(jnp.float32)
    y = x32 * lax.rsqrt(jnp.mean(x32 * x32, axis=-1, keepdims=True) + NORM_EPS)
    return (y * g.astype(jnp.float32)).astype(x.dtype)


def rope_tables(positions, rot_dim):
    inv_freq = ROPE_THETA ** (-jnp.arange(0, rot_dim, 2, dtype=jnp.float32) / rot_dim)
    ang = positions.astype(jnp.float32)[..., None] * inv_freq
    return jnp.cos(ang), jnp.sin(ang)


def apply_rope(x, cos, sin):
    shape = cos.shape[:2] + (1,) * (x.ndim - 3) + cos.shape[2:]
    cos = cos.reshape(shape)
    sin = sin.reshape(shape)
    x1, x2 = jnp.split(x, 2, axis=-1)
    return jnp.concatenate([x1 * cos - x2 * sin, x2 * cos + x1 * sin], axis=-1).astype(x.dtype)


def partial_rope(x, cos, sin, r):
    return jnp.concatenate([apply_rope(x[..., :r], cos, sin), x[..., r:]], axis=-1)


def to_blocks(t, blk):
    b, s = t.shape[:2]
    return jnp.moveaxis(t.reshape((b, s // blk, blk) + t.shape[2:]), 1, 0)


def from_blocks(t):
    n, b, blk = t.shape[:3]
    return jnp.moveaxis(t, 0, 1).reshape((b, n * blk) + t.shape[3:])


def chunk_causal_probs(q_blk, k, q_start, scale):
    s = jnp.einsum('bqhd,bkhd->bhqk', q_blk, k).astype(jnp.float32) * scale
    q_chunk = (q_start + jnp.arange(q_blk.shape[1])) // CHUNK
    k_chunk = jnp.arange(k.shape[1]) // CHUNK
    mask = k_chunk[None, :] <= q_chunk[:, None]
    return jax.nn.softmax(jnp.where(mask, s, -jnp.inf), axis=-1)


def blocked_softmax_attention(q, k, v, scale):
    nb = q.shape[1] // Q_BLOCK

    def body(args):
        q_blk, i = args
        p = chunk_causal_probs(q_blk, k, i * Q_BLOCK, scale)
        return jnp.einsum('bhqk,bkhd->bqhd', p.astype(v.dtype), v)

    return from_blocks(lax.map(body, (to_blocks(q, Q_BLOCK), jnp.arange(nb))))


def blocked_diff_attention(q1, k1, q2, k2, v, lam, scale):
    nb = q1.shape[1] // Q_BLOCK

    def body(args):
        q1_blk, q2_blk, i = args
        p1 = chunk_causal_probs(q1_blk, k1, i * Q_BLOCK, scale)
        p2 = chunk_causal_probs(q2_blk, k2, i * Q_BLOCK, scale)
        w = p1 - lam * p2
        return jnp.einsum('bhqk,bkhd->bqhd', w.astype(v.dtype), v)

    return from_blocks(lax.map(body, (to_blocks(q1, Q_BLOCK), to_blocks(q2, Q_BLOCK), jnp.arange(nb))))


def mixer_ab(h, w_in, q_norm_g, w_uq, kv_norm_g, w_ukv, lam_params, head_norm_g, w_out,
             rope_a, rope_b, lambda_init):
    bn, sn, _ = h.shape
    proj = h @ w_in
    q_lat, kv_lat, k_pe, bq, bk, bv = jnp.split(proj, AB_SPLITS, axis=-1)
    q = (rms_norm(q_lat, q_norm_g) @ w_uq).reshape(bn, sn, A_HEADS, A_NOPE + A_ROPE)
    q = jnp.concatenate([q[..., :A_NOPE], apply_rope(q[..., A_NOPE:], *rope_a)], axis=-1)
    kv = (rms_norm(kv_lat, kv_norm_g) @ w_ukv).reshape(bn, sn, A_HEADS, A_NOPE + A_V)
    k_pe = apply_rope(k_pe[:, :, None, :], *rope_a)
    k = jnp.concatenate([kv[..., :A_NOPE], jnp.broadcast_to(k_pe, (bn, sn, A_HEADS, A_ROPE))], axis=-1)
    a_out = blocked_softmax_attention(q, k, kv[..., A_NOPE:], (A_NOPE + A_ROPE) ** -0.5)
    bq = partial_rope(bq.reshape(bn, sn, B_HEADS, 2, B_DIM), *rope_b, B_ROPE)
    bk = partial_rope(bk.reshape(bn, sn, B_HEADS, 2, B_DIM), *rope_b, B_ROPE)
    bv = bv.reshape(bn, sn, B_HEADS, B_V)
    lp = lam_params.astype(jnp.float32)
    lam = jnp.exp(jnp.sum(lp[0] * lp[1])) - jnp.exp(jnp.sum(lp[2] * lp[3])) + lambda_init
    b_out = blocked_diff_attention(bq[..., 0, :], bk[..., 0, :], bq[..., 1, :], bk[..., 1, :],
                                   bv, lam, B_DIM ** -0.5)
    b_out = rms_norm(b_out, head_norm_g) * (1.0 - lambda_init)
    merged = jnp.concatenate([a_out.reshape(bn, sn, A_HEADS * A_V),
                              b_out.reshape(bn, sn, B_HEADS * B_V)], axis=-1)
    return merged @ w_out


def mixer_c(h, w_in, rel_bias, w_out, bias_idx):
    bn, sn, _ = h.shape
    qkv = (h @ w_in).reshape(bn, sn, 3, C_HEADS, C_DIM)
    q, k, v = qkv[:, :, 0], qkv[:, :, 1], qkv[:, :, 2]
    pad = ((0, 0), (C_PAST, 0), (0, 0), (0, 0))
    kp = jnp.pad(k, pad)
    vp = jnp.pad(v, pad)
    bias = rel_bias.astype(jnp.float32)[:, bias_idx]
    scale = C_DIM ** -0.5
    n_chunks = sn // CHUNK

    def body(args):
        q_blk, i = args
        start = i * CHUNK
        kb = lax.dynamic_slice_in_dim(kp, start, C_BAND, axis=1)
        vb = lax.dynamic_slice_in_dim(vp, start, C_BAND, axis=1)
        s = jnp.einsum('bqhd,bkhd->bhqk', q_blk, kb).astype(jnp.float32) * scale + bias
        valid = jnp.arange(C_BAND) >= C_PAST - start
        p = jax.nn.softmax(jnp.where(valid, s, -jnp.inf), axis=-1)
        return jnp.einsum('bhqk,bkhd->bqhd', p.astype(vb.dtype), vb)

    out = from_blocks(lax.map(body, (to_blocks(q, CHUNK), jnp.arange(n_chunks))))
    return out.reshape(bn, sn, D_MODEL) @ w_out


def peer_ffn(h, w_q, sub_keys, expert_u, expert_v):
    bn, sn, d = h.shape
    q = (h @ w_q).reshape(bn, sn, PEER_HEADS, 2, PEER_DQ // 2)
    scores = jnp.einsum('bshpd,hpnd->bshpn', q, sub_keys).astype(jnp.float32)
    s, i = lax.top_k(scores, PEER_TOPK)
    cand_s = (s[..., 0, :, None] + s[..., 1, None, :]).reshape(bn, sn, PEER_HEADS, PEER_TOPK * PEER_TOPK)
    cand_i = (i[..., 0, :, None] * N_KEYS + i[..., 1, None, :]).reshape(bn, sn, PEER_HEADS, PEER_TOPK * PEER_TOPK)
    top_s, pos = lax.top_k(cand_s, PEER_TOPK)
    idx = jnp.take_along_axis(cand_i, pos, axis=-1)
    gate = jax.nn.softmax(top_s, axis=-1)
    nb = (bn * sn) // PEER_TOKEN_BLOCK
    hb = h.reshape(nb, PEER_TOKEN_BLOCK, d)
    ib = idx.reshape(nb, PEER_TOKEN_BLOCK, PEER_HEADS * PEER_TOPK)
    gb = gate.reshape(nb, PEER_TOKEN_BLOCK, PEER_HEADS * PEER_TOPK)

    def body(args):
        ht, it, gt = args
        u = jnp.take(expert_u, it, axis=0)
        a = jax.nn.gelu(jnp.einsum('tkd,td->tk', u, ht).astype(jnp.float32), approximate=False) * gt
        vv = jnp.take(expert_v, it, axis=0)
        return jnp.einsum('tk,tkd->td', a.astype(vv.dtype), vv)

    return lax.map(body, (hb, ib, gb)).reshape(bn, sn, d)


def setup_inputs(seed: int = 0) -> dict:
    key = jax.random.key(seed)
    ks = jax.random.split(key, 24)
    ne = (DEPTH + 1) // 2
    no = DEPTH // 2

    def nrm(k, shape, scale):
        return jax.random.normal(k, shape, jnp.float32) * scale

    offsets = jax.random.randint(ks[2], (BATCH,), 0, 1024, dtype=jnp.int32) * CHUNK
    positions = (offsets[:, None] + jnp.arange(SEQ, dtype=jnp.int32)[None, :]).astype(jnp.int32)
    return {
        'x': nrm(ks[0], (BATCH, SEQ, D_MODEL), 1.0),
        'c': nrm(ks[1], (BATCH, D_MODEL), 1.0),
        'positions': positions,
        'ada_w': nrm(ks[3], (DEPTH, D_MODEL, 6 * D_MODEL), 0.5 * D_MODEL ** -0.5),
        'ada_b': nrm(ks[4], (DEPTH, 6 * D_MODEL), 0.01),
        'norm_mix_g': 1.0 + nrm(ks[5], (DEPTH, D_MODEL), 0.05),
        'norm_ffn_g': 1.0 + nrm(ks[6], (DEPTH, D_MODEL), 0.05),
        'w_in_ab': nrm(ks[7], (ne, D_MODEL, AB_IN), D_MODEL ** -0.5),
        'mla_q_norm_g': 1.0 + nrm(ks[8], (ne, A_Q_LORA), 0.05),
        'mla_w_uq': nrm(ks[9], (ne, A_Q_LORA, A_HEADS * (A_NOPE + A_ROPE)), A_Q_LORA ** -0.5),
        'mla_kv_norm_g': 1.0 + nrm(ks[10], (ne, A_KV_LORA), 0.05),
        'mla_w_ukv': nrm(ks[11], (ne, A_KV_LORA, A_HEADS * (A_NOPE + A_V)), A_KV_LORA ** -0.5),
        'diff_lambda': nrm(ks[12], (ne, 4, B_DIM), 0.1),
        'diff_head_norm_g': 1.0 + nrm(ks[13], (ne, B_V), 0.05),
        'w_out_ab': nrm(ks[14], (ne, AB_OUT, D_MODEL), AB_OUT ** -0.5),
        'w_in_c': nrm(ks[15], (no, D_MODEL, 3 * D_MODEL), D_MODEL ** -0.5),
        'rel_bias': nrm(ks[16], (no, C_HEADS, 2 * REL_CLIP + 1), 0.2),
        'w_out_c': nrm(ks[17], (no, D_MODEL, D_MODEL), D_MODEL ** -0.5),
        'peer_w_q': nrm(ks[18], (DEPTH, D_MODEL, PEER_HEADS * PEER_DQ), D_MODEL ** -0.5),
        'peer_sub_keys': nrm(ks[19], (DEPTH, PEER_HEADS, 2, N_KEYS, PEER_DQ // 2), (PEER_DQ // 2) ** -0.5),
        'peer_u': nrm(ks[20], (DEPTH, N_EXPERTS, D_MODEL), D_MODEL ** -0.5),
        'peer_v': nrm(ks[21], (DEPTH, N_EXPERTS, D_MODEL), PEER_HEADS ** -0.5),
        'final_norm_g': 1.0 + nrm(ks[22], (D_MODEL,), 0.05),
    }


def reference(x, c, positions, ada_w, ada_b, norm_mix_g, norm_ffn_g, w_in_ab, mla_q_norm_g,
              mla_w_uq, mla_kv_norm_g, mla_w_ukv, diff_lambda, diff_head_norm_g, w_out_ab,
              w_in_c, rel_bias, w_out_c, peer_w_q, peer_sub_keys, peer_u, peer_v, final_norm_g):
    rope_a = rope_tables(positions, A_ROPE)
    rope_b = rope_tables(positions, B_ROPE)
    rel = jnp.arange(CHUNK)[:, None] + C_PAST - jnp.arange(C_BAND)[None, :]
    bias_idx = jnp.clip(rel, -REL_CLIP, REL_CLIP) + REL_CLIP
    cond = jax.nn.silu(c)
    for l in range(DEPTH):
        mod = cond @ ada_w[l] + ada_b[l]
        sh_m, sc_m, g_m, sh_f, sc_f, g_f = jnp.split(mod[:, None, :], 6, axis=-1)
        h = rms_norm(x, norm_mix_g[l]) * (1.0 + sc_m) + sh_m
        j = l // 2
        if l % 2 == 0:
            lambda_init = 0.8 - 0.6 * math.exp(-0.3 * l)
            y = mixer_ab(h, w_in_ab[j], mla_q_norm_g[j], mla_w_uq[j], mla_kv_norm_g[j], mla_w_ukv[j],
                         diff_lambda[j], diff_head_norm_g[j], w_out_ab[j], rope_a, rope_b, lambda_init)
        else:
            y = mixer_c(h, w_in_c[j], rel_bias[j], w_out_c[j], bias_idx)
        x = x + g_m * y
        h = rms_norm(x, norm_ffn_g[l]) * (1.0 + sc_f) + sh_f
        x = x + g_f * peer_ffn(h, peer_w_q[l], peer_sub_keys[l], peer_u[l], peer_v[l])
    return rms_norm(x, final_norm_g)
```

```python
import functools
import math

import jax
import jax.numpy as jnp
from jax import lax
from jax.experimental import pallas as pl
from jax.experimental.pallas import tpu as pltpu

F32 = jnp.float32
BF16 = jnp.bfloat16
I32 = jnp.int32
U32 = jnp.uint32

CHUNK = 64
ROPE_THETA = 500000.0
NORM_EPS = 1e-6
A_HEADS, A_NOPE, A_ROPE, A_V = 8, 64, 32, 64
A_Q_LORA, A_KV_LORA = 256, 128
B_HEADS, B_DIM = 8, 32
B_V = 2 * B_DIM
B_ROPE = B_DIM // 4
C_HEADS = 16
C_PAST = 8 * CHUNK
REL_CLIP = 128
PEER_HEADS, N_KEYS, PEER_TOPK = 8, 128, 16
HALF_EXPERTS = N_KEYS * N_KEYS // 2

LANES = 128
SUBLANES = 8
VMEM_LIMIT = 56 * 1024 * 1024
TILE_STRIDE = 136

_NT = (((1,), (1,)), ((), ()))


def _cp(n_axes):
    return pltpu.CompilerParams(dimension_semantics=("arbitrary",) * n_axes,
                                vmem_limit_bytes=VMEM_LIMIT)


def _rms(x, g):
    return x * lax.rsqrt(jnp.mean(x * x, axis=-1, keepdims=True) + NORM_EPS) * g


def _ada_norm(x, g, sc, sh):
    return _rms(x, g) * (1.0 + sc) + sh


def _split_bf16(a):
    hi = a.astype(BF16)
    lo = (a - hi.astype(F32)).astype(BF16)
    return hi, lo


def _dot3(a, b_hi, b_lo, dims):
    a_hi, a_lo = _split_bf16(a)
    f = lambda u, v: lax.dot_general(u, v, dims, preferred_element_type=F32)
    return f(a_hi, b_hi) + (f(a_lo, b_hi) + f(a_hi, b_lo))


def _mod_kernel(c_ref, w_ref, b_ref, o_ref):
    c = c_ref[...]
    cond = c * jax.nn.sigmoid(c)
    o_ref[0] = jnp.dot(cond, w_ref[0], preferred_element_type=F32,
                       precision=lax.Precision.HIGHEST) + b_ref[0]


def _modulation(c, ada_w, ada_b):
    depth, d, n = ada_w.shape
    bn = c.shape[0]
    tn = 512
    return pl.pallas_call(
        _mod_kernel,
        grid=(depth, n // tn),
        in_specs=[pl.BlockSpec((bn, d), lambda l, j: (0, 0)),
                  pl.BlockSpec((1, d, tn), lambda l, j: (l, 0, j)),
                  pl.BlockSpec((1, 1, tn), lambda l, j: (l, 0, j))],
        out_specs=pl.BlockSpec((1, bn, tn), lambda l, j: (l, 0, j)),
        out_shape=jax.ShapeDtypeStruct((depth, bn, n), F32),
        compiler_params=_cp(2),
        name="modulation",
    )(c, ada_w, ada_b.reshape(depth, 1, n))


def _ab_proj_kernel(x_ref, g_ref, sc_ref, sh_ref, w1_ref, gq_ref, wq_ref, gkv_ref, wkv_ref,
                    ca_ref, sa_ref, cb_ref, sb_ref,
                    qa_ref, ka_ref, va_ref, qb_ref, kb_ref, vb_ref):
    h = _ada_norm(x_ref[0], g_ref[...], sc_ref[0], sh_ref[0])
    proj = jnp.dot(h.astype(BF16), w1_ref[...], preferred_element_type=F32)
    o = 0
    q_lat = proj[:, o:o + A_Q_LORA]; o += A_Q_LORA
    kv_lat = proj[:, o:o + A_KV_LORA]; o += A_KV_LORA
    kpe_m = proj[:, o:o + LANES]; o += LANES
    kpe_s = proj[:, o:o + LANES]; o += LANES
    nb = B_HEADS * 2 * B_DIM
    bq_m = proj[:, o:o + nb]; o += nb
    bq_s = proj[:, o:o + nb]; o += nb
    bk_m = proj[:, o:o + nb]; o += nb
    bk_s = proj[:, o:o + nb]; o += nb
    bv = proj[:, o:o + nb]

    ca, sa = ca_ref[0], sa_ref[0]
    cb, sb = cb_ref[0], sb_ref[0]

    qq = jnp.dot(_rms(q_lat, gq_ref[...]).astype(BF16), wq_ref[...], preferred_element_type=F32)
    scale_a = (A_NOPE + A_ROPE) ** -0.5
    nq = A_HEADS * LANES
    for hd in range(A_HEADS):
        m = qq[:, hd * LANES:(hd + 1) * LANES]
        s = qq[:, nq + hd * LANES:nq + (hd + 1) * LANES]
        qa_ref[0, hd] = ((m * ca + s * sa) * scale_a).astype(BF16)

    kv = jnp.dot(_rms(kv_lat, gkv_ref[...]).astype(BF16), wkv_ref[...], preferred_element_type=F32)
    kpe = kpe_m * ca + kpe_s * sa
    for hd in range(A_HEADS):
        ka_ref[0, hd] = (kv[:, hd * LANES:(hd + 1) * LANES] + kpe).astype(BF16)
    for j in range(A_HEADS // 2):
        va_ref[0, j] = kv[:, nq + j * LANES:nq + (j + 1) * LANES].astype(BF16)

    scale_b = B_DIM ** -0.5
    for j in range(B_HEADS // 2):
        sl = slice(j * LANES, (j + 1) * LANES)
        qb_ref[0, j] = ((bq_m[:, sl] * cb + bq_s[:, sl] * sb) * scale_b).astype(BF16)
        kb_ref[0, j] = (bk_m[:, sl] * cb + bk_s[:, sl] * sb).astype(BF16)
        vb_ref[0, j] = bv[:, sl].astype(BF16)


def _ab_proj(x, g, sc, sh, w1, gq, wq, gkv, wkv, ca, sa, cb, sb, ts=256):
    bn, sn, d = x.shape
    full = lambda a: pl.BlockSpec(a.shape, lambda b, i: (0,) * a.ndim)
    tok = lambda w: pl.BlockSpec((1, ts, w), lambda b, i: (b, i, 0))
    per_b = pl.BlockSpec((1, 1, d), lambda b, i: (b, 0, 0))
    heads = lambda n: pl.BlockSpec((1, n, ts, LANES), lambda b, i: (b, 0, i, 0))
    oshape = lambda n: jax.ShapeDtypeStruct((bn, n, sn, LANES), BF16)
    return pl.pallas_call(
        _ab_proj_kernel,
        grid=(bn, sn // ts),
        in_specs=[tok(d), full(g), per_b, per_b, full(w1), full(gq), full(wq), full(gkv), full(wkv),
                  tok(LANES), tok(LANES), tok(LANES), tok(LANES)],
        out_specs=[heads(8), heads(8), heads(4), heads(4), heads(4), heads(4)],
        out_shape=[oshape(8), oshape(8), oshape(4), oshape(4), oshape(4), oshape(4)],
        compiler_params=_cp(2),
        name="ab_proj",
    )(x, g, sc, sh, w1, gq, wq, gkv, wkv, ca, sa, cb, sb)


def _chunk_causal_mask(tq):
    r = lax.broadcasted_iota(I32, (tq, tq), 0) // CHUNK
    c = lax.broadcasted_iota(I32, (tq, tq), 1) // CHUNK
    return c <= r


def _flash_update(idx, s, v_blk, m_sc, l_sc, acc_sc):
    m_prev = m_sc[idx]
    m_new = jnp.maximum(m_prev, jnp.max(s, axis=1, keepdims=True))
    alpha = jnp.exp(m_prev - m_new)
    p = jnp.exp(s - m_new[:, :1])
    l_sc[idx] = alpha * l_sc[idx] + jnp.sum(p, axis=1, keepdims=True)
    acc_sc[idx] = alpha * acc_sc[idx] + jnp.dot(p.astype(BF16), v_blk, preferred_element_type=F32)
    m_sc[idx] = m_new


def _mla_kernel(q_ref, k_ref, v_ref, o_ref, m_sc, l_sc, acc_sc, *, tq):
    i = pl.program_id(2)
    m_sc[...] = jnp.full(m_sc.shape, -jnp.inf, F32)
    l_sc[...] = jnp.zeros(l_sc.shape, F32)
    acc_sc[...] = jnp.zeros(acc_sc.shape, F32)

    def step(kb, mask):
        ks = pl.multiple_of(kb * tq, tq)
        v_blk = v_ref[0, 0, pl.ds(ks, tq), :]
        for hd in range(2):
            s = lax.dot_general(q_ref[0, hd], k_ref[0, hd, pl.ds(ks, tq), :], _NT,
                                preferred_element_type=F32)
            if mask is not None:
                s = jnp.where(mask, s, -jnp.inf)
            _flash_update(hd, s, v_blk, m_sc, l_sc, acc_sc)

    def body(kb, carry):
        step(kb, None)
        return carry

    lax.fori_loop(0, i, body, 0)
    step(i, _chunk_causal_mask(tq))

    lane = lax.broadcasted_iota(I32, (tq, LANES), 1)
    o0 = acc_sc[0] / l_sc[0]
    o1 = acc_sc[1] / l_sc[1]
    o_ref[0] = jnp.where(lane < A_V, o0, o1).astype(BF16)


def _mla_attention(qa, ka, va, tq=256):
    bn, _, sn, _ = qa.shape
    npair = A_HEADS // 2
    return pl.pallas_call(
        functools.partial(_mla_kernel, tq=tq),
        grid=(bn, npair, sn // tq),
        in_specs=[pl.BlockSpec((1, 2, tq, LANES), lambda b, j, i: (b, j, i, 0)),
                  pl.BlockSpec((1, 2, sn, LANES), lambda b, j, i: (b, j, 0, 0)),
                  pl.BlockSpec((1, 1, sn, LANES), lambda b, j, i: (b, j, 0, 0))],
        out_specs=pl.BlockSpec((1, tq, LANES), lambda b, j, i: (b, i, j)),
        out_shape=jax.ShapeDtypeStruct((bn, sn, npair * LANES), BF16),
        scratch_shapes=[pltpu.VMEM((2, tq, LANES), F32)] * 3,
        compiler_params=_cp(3),
        name="mla_attention",
    )(qa, ka, va)


def _diff_kernel(lp_ref, hg_ref, q_ref, k_ref, v_ref, o_ref, m_sc, l_sc, acc_sc, *, tq, lambda_init):
    i = pl.program_id(2)
    m_sc[...] = jnp.full(m_sc.shape, -jnp.inf, F32)
    l_sc[...] = jnp.zeros(l_sc.shape, F32)
    acc_sc[...] = jnp.zeros(acc_sc.shape, F32)

    lane = lax.broadcasted_iota(I32, (tq, LANES), 1)
    q = q_ref[0, 0]
    zero = jnp.zeros_like(q)
    qs = [jnp.where((lane >= c * B_DIM) & (lane < (c + 1) * B_DIM), q, zero) for c in range(4)]

    def step(kb, mask):
        ks = pl.multiple_of(kb * tq, tq)
        k_blk = k_ref[0, 0, pl.ds(ks, tq), :]
        v_blk = v_ref[0, 0, pl.ds(ks, tq), :]
        for c in range(4):
            s = lax.dot_general(qs[c], k_blk, _NT, preferred_element_type=F32)
            if mask is not None:
                s = jnp.where(mask, s, -jnp.inf)
            _flash_update(c, s, v_blk, m_sc, l_sc, acc_sc)

    def body(kb, carry):
        step(kb, None)
        return carry

    lax.fori_loop(0, i, body, 0)
    step(i, _chunk_causal_mask(tq))

    lp = lp_ref[...]
    lam = (jnp.exp(jnp.sum(lp[0:1] * lp[1:2], keepdims=True))
           - jnp.exp(jnp.sum(lp[2:3] * lp[3:4], keepdims=True)) + lambda_init)
    d0 = acc_sc[0] / l_sc[0] - lam * (acc_sc[1] / l_sc[1])
    d1 = acc_sc[2] / l_sc[2] - lam * (acc_sc[3] / l_sc[3])
    lo = lane < B_V
    d = jnp.where(lo, d0, d1)
    dd = d * d
    ms_lo = jnp.sum(jnp.where(lo, dd, 0.0), axis=1, keepdims=True) * (1.0 / B_V)
    ms_hi = jnp.sum(jnp.where(lo, 0.0, dd), axis=1, keepdims=True) * (1.0 / B_V)
    ms = jnp.where(lo, ms_lo, ms_hi)
    y = d * lax.rsqrt(ms + NORM_EPS) * hg_ref[...]
    o_ref[0] = (y * (1.0 - lambda_init)).astype(BF16)


def _diff_attention(lam_params, head_g, qb, kb, vb, lambda_init, tq=256):
    bn, npair, sn, _ = qb.shape
    hg = jnp.tile(head_g, 2).reshape(1, LANES)
    kv_spec = pl.BlockSpec((1, 1, sn, LANES), lambda b, j, i: (b, j, 0, 0))
    return pl.pallas_call(
        functools.partial(_diff_kernel, tq=tq, lambda_init=lambda_init),
        grid=(bn, npair, sn // tq),
        in_specs=[pl.BlockSpec(lam_params.shape, lambda b, j, i: (0, 0)),
                  pl.BlockSpec((1, LANES), lambda b, j, i: (0, 0)),
                  pl.BlockSpec((1, 1, tq, LANES), lambda b, j, i: (b, j, i, 0)),
                  kv_spec, kv_spec],
        out_specs=pl.BlockSpec((1, tq, LANES), lambda b, j, i: (b, i, j)),
        out_shape=jax.ShapeDtypeStruct((bn, sn, npair * LANES), BF16),
        scratch_shapes=[pltpu.VMEM((4, tq, LANES), F32)] * 3,
        compiler_params=_cp(3),
        name="diff_attention",
    )(lam_params, hg, qb, kb, vb)


def _out_proj_kernel(*refs, n_in):
    x_ref, gate_ref = refs[0], refs[1]
    o_ref = refs[2 + 2 * n_in]
    y = None
    for t in range(n_in):
        part = jnp.dot(refs[2 + 2 * t][0], refs[3 + 2 * t][...], preferred_element_type=F32)
        y = part if y is None else y + part
    o_ref[0] = x_ref[0] + gate_ref[0] * y


def _out_proj(x, gate, acts, weights, ts=512):
    bn, sn, d = x.shape
    in_specs = [pl.BlockSpec((1, ts, d), lambda b, i: (b, i, 0)),
                pl.BlockSpec((1, 1, d), lambda b, i: (b, 0, 0))]
    args = [x, gate]
    for a, w in zip(acts, weights):
        in_specs.append(pl.BlockSpec((1, ts, a.shape[-1]), lambda b, i: (b, i, 0)))
        in_specs.append(pl.BlockSpec(w.shape, lambda b, i: (0, 0)))
        args += [a, w]
    return pl.pallas_call(
        functools.partial(_out_proj_kernel, n_in=len(acts)),
        grid=(bn, sn // ts),
        in_specs=in_specs,
        out_specs=pl.BlockSpec((1, ts, d), lambda b, i: (b, i, 0)),
        out_shape=jax.ShapeDtypeStruct((bn, sn, d), F32),
        compiler_params=_cp(2),
        name="out_proj",
    )(*args)


def _c_proj_kernel(x_ref, g_ref, sc_ref, sh_ref, w_ref, q_ref, k_ref, v_ref):
    h = _ada_norm(x_ref[0], g_ref[...], sc_ref[0], sh_ref[0])
    qkv = jnp.dot(h.astype(BF16), w_ref[...], preferred_element_type=F32)
    d = x_ref.shape[-1]
    scale = (d // C_HEADS) ** -0.5
    for j in range(d // LANES):
        sl = lambda base: slice(base + j * LANES, base + (j + 1) * LANES)
        q_ref[0, j] = (qkv[:, sl(0)] * scale).astype(BF16)
        k_ref[0, j] = qkv[:, sl(d)].astype(BF16)
        v_ref[0, j] = qkv[:, sl(2 * d)].astype(BF16)


def _c_proj(x, g, sc, sh, w, ts=256):
    bn, sn, d = x.shape
    npair = d // LANES
    per_b = pl.BlockSpec((1, 1, d), lambda b, i: (b, 0, 0))
    heads = pl.BlockSpec((1, npair, ts, LANES), lambda b, i: (b, 0, i, 0))
    oshape = jax.ShapeDtypeStruct((bn, npair, sn, LANES), BF16)
    return pl.pallas_call(
        _c_proj_kernel,
        grid=(bn, sn // ts),
        in_specs=[pl.BlockSpec((1, ts, d), lambda b, i: (b, i, 0)),
                  pl.BlockSpec(g.shape, lambda b, i: (0, 0)), per_b, per_b,
                  pl.BlockSpec(w.shape, lambda b, i: (0, 0))],
        out_specs=[heads, heads, heads],
        out_shape=[oshape, oshape, oshape],
        compiler_params=_cp(2),
        name="c_proj",
    )(x, g, sc, sh, w)


def _bias_kernel(rb_ref, o_ref, *, width, rpad):
    base = pl.program_id(0) * SUBLANES
    kk = lax.broadcasted_iota(I32, (rpad, width), 1)
    r = lax.broadcasted_iota(I32, (rpad, width), 0)
    kk1 = lax.broadcasted_iota(I32, (1, width), 1)
    rb = rb_ref[...]
    for qi in range(SUBLANES):
        q = base + qi
        idx = jnp.clip(q + C_PAST - kk, -REL_CLIP, REL_CLIP) + REL_CLIP
        onehot = (r == idx).astype(F32)
        row = jnp.dot(rb, onehot, preferred_element_type=F32, precision=lax.Precision.HIGHEST)
        qc = q // CHUNK
        kc = kk1 // CHUNK
        ok = (kc >= qc) & (kc <= qc + C_PAST // CHUNK)
        o_ref[qi] = jnp.where(ok, row, -jnp.inf)


def _band_bias(rel_bias, tq):
    nh, nr = rel_bias.shape
    rpad = 3 * LANES
    width = tq + C_PAST
    rb = jnp.pad(rel_bias, ((0, 0), (0, rpad - nr)))
    return pl.pallas_call(
        functools.partial(_bias_kernel, width=width, rpad=rpad),
        grid=(tq // SUBLANES,),
        in_specs=[pl.BlockSpec((nh, rpad), lambda i: (0, 0))],
        out_specs=pl.BlockSpec((SUBLANES, nh, width), lambda i: (i, 0, 0)),
        out_shape=jax.ShapeDtypeStruct((tq, nh, width), F32),
        compiler_params=_cp(1),
        name="band_bias",
    )(rb).transpose(1, 0, 2)


def _chunk_attn_kernel(q_ref, k_ref, v_ref, b_ref, o_ref, *, tq, width):
    i = pl.program_id(2)
    q0 = pl.multiple_of(i * tq, tq)
    k_blk = k_ref[0, 0, pl.ds(q0, width), :]
    v_blk = v_ref[0, 0, pl.ds(q0, width), :]
    in_seq = lax.broadcasted_iota(I32, (1, width), 1) + q0 >= C_PAST
    lane = lax.broadcasted_iota(I32, (tq, LANES), 1)
    q = q_ref[0, 0]
    zero = jnp.zeros_like(q)
    outs = []
    for hd in range(2):
        hmask = (lane >= hd * CHUNK) & (lane < (hd + 1) * CHUNK)
        s = lax.dot_general(jnp.where(hmask, q, zero), k_blk, _NT, preferred_element_type=F32)
        s = jnp.where(in_seq, s + b_ref[hd], -jnp.inf)
        m = jnp.max(s, axis=1, keepdims=True)
        p = jnp.exp(s - m)
        l = jnp.sum(p, axis=1, keepdims=True)
        outs.append(jnp.dot(p.astype(BF16), v_blk, preferred_element_type=F32) / l)
    o_ref[0] = jnp.where(lane < CHUNK, outs[0], outs[1]).astype(BF16)


def _chunk_attention(q, kp, vp, bias, tq):
    bn, npair, sn, _ = q.shape
    width = tq + C_PAST
    kv_spec = pl.BlockSpec((1, 1, sn + C_PAST, LANES), lambda b, j, i: (b, j, 0, 0))
    return pl.pallas_call(
        functools.partial(_chunk_attn_kernel, tq=tq, width=width),
        grid=(bn, npair, sn // tq),
        in_specs=[pl.BlockSpec((1, 1, tq, LANES), lambda b, j, i: (b, j, i, 0)),
                  kv_spec, kv_spec,
                  pl.BlockSpec((2, tq, width), lambda b, j, i: (j, 0, 0))],
        out_specs=pl.BlockSpec((1, tq, LANES), lambda b, j, i: (b, i, j)),
        out_shape=jax.ShapeDtypeStruct((bn, sn, npair * LANES), BF16),
        compiler_params=_cp(3),
        name="chunk_attention",
    )(q, kp, vp, bias)


def _topk_rows(s, k, payload=None):
    nrows = s.shape[0]
    rows = lax.broadcasted_iota(I32, s.shape, 0)
    vals, idxs, picked = [], [], []
    for _ in range(k):
        m = jnp.max(s, axis=0, keepdims=True)
        first = jnp.min(jnp.where(s == m, rows, nrows), axis=0, keepdims=True)
        hit = rows == first
        vals.append(m)
        idxs.append(first)
        if payload is not None:
            picked.append(jnp.sum(jnp.where(hit, payload, 0), axis=0, keepdims=True))
        s = jnp.where(hit, -jnp.inf, s)
    cat = lambda xs: jnp.concatenate(xs, axis=0)
    return cat(vals), cat(idxs), (cat(picked) if payload is not None else None)


def _route_kernel(x_ref, g_ref, sc_ref, sh_ref, wh_ref, wl_ref, keys_ref,
                  h_ref, off_ref, shift_ref, gate_ref,
                  q_sc, s1_sc, i1_sc, ts_sc, ti_sc, *, tt):
    h = _ada_norm(x_ref[0], g_ref[...], sc_ref[0], sh_ref[0])
    h_ref[0] = h
    q = _dot3(h, wh_ref[...], wl_ref[...], (((1,), (0,)), ((), ())))
    n_hp = 2 * PEER_HEADS
    for hp in range(n_hp):
        q_sc[hp] = q[:, hp * N_KEYS:(hp + 1) * N_KEYS]

    def stage1(hp, carry):
        k_hi, k_lo = _split_bf16(keys_ref[hp])
        q_hi, q_lo = _split_bf16(q_sc[hp])
        f = lambda u, v: lax.dot_general(u, v, _NT, preferred_element_type=F32)
        sc_t = f(k_hi, q_hi) + (f(k_lo, q_hi) + f(k_hi, q_lo))
        vals, idxs, _ = _topk_rows(sc_t, PEER_TOPK)
        s1_sc[hp] = vals
        i1_sc[hp] = idxs
        return carry

    lax.fori_loop(0, n_hp, stage1, 0)

    def stage2(hd, carry):
        s_a, s_b = s1_sc[2 * hd], s1_sc[2 * hd + 1]
        i_a, i_b = i1_sc[2 * hd], i1_sc[2 * hd + 1]
        cand_s = jnp.concatenate([s_a[a:a + 1] + s_b for a in range(PEER_TOPK)], axis=0)
        cand_i = jnp.concatenate([i_a[a:a + 1] * N_KEYS + i_b for a in range(PEER_TOPK)], axis=0)
        vals, _, ids = _topk_rows(cand_s, PEER_TOPK, payload=cand_i)
        rows = pl.ds(pl.multiple_of(hd * PEER_TOPK, PEER_TOPK), PEER_TOPK)
        ts_sc[rows, :] = vals
        ti_sc[rows, :] = ids
        return carry

    lax.fori_loop(0, PEER_HEADS, stage2, 0)

    for hd in range(PEER_HEADS):
        rows = slice(hd * PEER_TOPK, (hd + 1) * PEER_TOPK)
        v = ts_sc[rows, :]
        e = jnp.exp(v - v[0:1])
        ts_sc[rows, :] = e / jnp.sum(e, axis=0, keepdims=True)

    ids = ti_sc[...]
    off = (ids & (HALF_EXPERTS - 1)) * SUBLANES
    shift = (1 - (ids >> 13)) * 16
    gates = ts_sc[...]
    for c in range(tt // LANES):
        cols = slice(c * LANES, (c + 1) * LANES)
        off_ref[cols, :] = off[:, cols].T
        shift_ref[cols, :] = shift[:, cols].T
        gate_ref[cols, :] = gates[:, cols].T


def _peer_route(x, g, sc, sh, w_hi, w_lo, keys, tt=256):
    bn, sn, d = x.shape
    n_hp = 2 * PEER_HEADS
    nsel = PEER_HEADS * PEER_TOPK
    per_b = pl.BlockSpec((1, 1, d), lambda b, i: (b, 0, 0))
    nblk = sn // tt
    tok2 = pl.BlockSpec((tt, nsel), lambda b, i: (b * nblk + i, 0))
    return pl.pallas_call(
        functools.partial(_route_kernel, tt=tt),
        grid=(bn, nblk),
        in_specs=[pl.BlockSpec((1, tt, d), lambda b, i: (b, i, 0)),
                  pl.BlockSpec(g.shape, lambda b, i: (0, 0)), per_b, per_b,
                  pl.BlockSpec(w_hi.shape, lambda b, i: (0, 0)),
                  pl.BlockSpec(w_lo.shape, lambda b, i: (0, 0)),
                  pl.BlockSpec(keys.shape, lambda b, i: (0, 0, 0))],
        out_specs=[pl.BlockSpec((1, tt, d), lambda b, i: (b, i, 0)), tok2, tok2, tok2],
        out_shape=[jax.ShapeDtypeStruct((bn, sn, d), F32),
                   jax.ShapeDtypeStruct((bn * sn, nsel), I32),
                   jax.ShapeDtypeStruct((bn * sn, nsel), I32),
                   jax.ShapeDtypeStruct((bn * sn, nsel), F32)],
        scratch_shapes=[pltpu.VMEM((n_hp, tt, N_KEYS), F32),
                        pltpu.VMEM((n_hp, PEER_TOPK, tt), F32),
                        pltpu.VMEM((n_hp, PEER_TOPK, tt), I32),
                        pltpu.VMEM((nsel, tt), F32),
                        pltpu.VMEM((nsel, tt), I32)],
        compiler_params=_cp(2),
        name="peer_route",
    )(x, g, sc, sh, w_hi, w_lo, keys)


def _unpack_row(tab_ref, off, shift):
    w = tab_ref[pl.ds(pl.multiple_of(off, SUBLANES), SUBLANES), :]
    bits = lax.shift_left(w, jnp.full(w.shape, shift, I32).astype(U32)) & jnp.uint32(0xFFFF0000)
    return lax.bitcast_convert_type(bits, F32)


def _peer_u_kernel(off_ref, shift_ref, h_ref, gate_ref, tab_ref, a_ref, tile_sc, s_sc, *, tb):
    nsel = PEER_HEADS * PEER_TOPK
    ones = jnp.ones((SUBLANES, LANES), BF16)

    def token(t, carry):
        hv = h_ref[pl.ds(pl.multiple_of(t * SUBLANES, SUBLANES), SUBLANES), :]
        for k in range(nsel):
            row = _unpack_row(tab_ref, off_ref[t, k], shift_ref[t, k])
            tile_sc[pl.ds(k, SUBLANES, stride=TILE_STRIDE), :] = row * hv
        part = tile_sc[0:nsel, :]
        for j in range(1, SUBLANES):
            part = part + tile_sc[j * TILE_STRIDE:j * TILE_STRIDE + nsel, :]
        p_hi, p_lo = _split_bf16(part)
        f = lambda v: lax.dot_general(ones, v, _NT, preferred_element_type=F32)
        s_sc[pl.ds(t, 1), :] = (f(p_hi) + f(p_lo))[0:1]
        return carry

    lax.fori_loop(0, tb, token, 0)
    s = s_sc[...]
    gelu = 0.5 * s * (1.0 + lax.erf(s * math.sqrt(0.5)))
    a_ref[...] = gelu * gate_ref[...]


def _peer_u(off, shift, h8, gate, table, tb=64):
    nt, nsel = off.shape
    smem = pl.BlockSpec((tb, nsel), lambda i: (i, 0), memory_space=pltpu.SMEM)
    return pl.pallas_call(
        functools.partial(_peer_u_kernel, tb=tb),
        grid=(nt // tb,),
        in_specs=[smem, smem,
                  pl.BlockSpec((tb * SUBLANES, LANES), lambda i: (i, 0)),
                  pl.BlockSpec((tb, nsel), lambda i: (i, 0)),
                  pl.BlockSpec(table.shape, lambda i: (0, 0), pipeline_mode=pl.Buffered(1))],
        out_specs=pl.BlockSpec((tb, nsel), lambda i: (i, 0)),
        out_shape=jax.ShapeDtypeStruct((nt, nsel), F32),
        scratch_shapes=[pltpu.VMEM((SUBLANES * TILE_STRIDE, LANES), F32),
                        pltpu.VMEM((tb, nsel), F32)],
        compiler_params=_cp(1),
        name="peer_u",
    )(off, shift, h8, gate, table)


def _peer_v_kernel(off_ref, shift_ref, a_ref, x_ref, gf_ref, tab_ref, o_ref, *, tb):
    nsel = PEER_HEADS * PEER_TOPK
    n_acc = 4

    def token(t, carry):
        accs = [jnp.zeros((SUBLANES, LANES), F32) for _ in range(n_acc)]
        for k in range(nsel):
            row = _unpack_row(tab_ref, off_ref[t, k], shift_ref[t, k])
            accs[k % n_acc] = accs[k % n_acc] + a_ref[t, k] * row
        y = (accs[0] + accs[1]) + (accs[2] + accs[3])
        rows = pl.ds(pl.multiple_of(t * SUBLANES, SUBLANES), SUBLANES)
        o_ref[rows, :] = x_ref[rows, :] + gf_ref[...] * y
        return carry

    lax.fori_loop(0, tb, token, 0)


def _peer_v(off, shift, a, x8, gf8, table, tokens_per_batch, tb=64):
    nt, nsel = off.shape
    smem = pl.BlockSpec((tb, nsel), lambda i: (i, 0), memory_space=pltpu.SMEM)
    blocks_per_batch = tokens_per_batch // tb
    return pl.pallas_call(
        functools.partial(_peer_v_kernel, tb=tb),
        grid=(nt // tb,),
        in_specs=[smem, smem, smem,
                  pl.BlockSpec((tb * SUBLANES, LANES), lambda i: (i, 0)),
                  pl.BlockSpec((SUBLANES, LANES), lambda i: (i // blocks_per_batch, 0)),
                  pl.BlockSpec(table.shape, lambda i: (0, 0), pipeline_mode=pl.Buffered(1))],
        out_specs=pl.BlockSpec((tb * SUBLANES, LANES), lambda i: (i, 0)),
        out_shape=jax.ShapeDtypeStruct(x8.shape, F32),
        compiler_params=_cp(1),
        name="peer_v",
    )(off, shift, a, x8, gf8, table)


def _pack_table(t):
    n, d = t.shape
    bits = lax.bitcast_convert_type(t.astype(BF16), jnp.uint16).astype(U32)
    packed = bits[:n // 2] | (bits[n // 2:] << 16)
    return packed.reshape(n // 2 * (d // LANES), LANES)


def _peer_ffn(x, g, sc, sh, gate_f, w_q, sub_keys, tab_u, tab_v):
    bn, sn, d = x.shape
    w_hi = w_q.astype(BF16)
    w_lo = (w_q - w_hi.astype(F32)).astype(BF16)
    keys = sub_keys.reshape(2 * PEER_HEADS, N_KEYS, -1)
    h, off, shift, gate = _peer_route(x, g, sc, sh, w_hi, w_lo, keys)
    rows = d // LANES
    a = _peer_u(off, shift, h.reshape(bn * sn * rows, LANES), gate, tab_u)
    out8 = _peer_v(off, shift, a, x.reshape(bn * sn * rows, LANES),
                   gate_f.reshape(bn * rows, LANES), tab_v, sn)
    return out8.reshape(bn, sn, d)


def _final_norm_kernel(x_ref, g_ref, o_ref):
    o_ref[0] = _rms(x_ref[0], g_ref[...])


def _final_norm(x, g, ts=512):
    bn, sn, d = x.shape
    return pl.pallas_call(
        _final_norm_kernel,
        grid=(bn, sn // ts),
        in_specs=[pl.BlockSpec((1, ts, d), lambda b, i: (b, i, 0)),
                  pl.BlockSpec((1, d), lambda b, i: (0, 0))],
        out_specs=pl.BlockSpec((1, ts, d), lambda b, i: (b, i, 0)),
        out_shape=jax.ShapeDtypeStruct((bn, sn, d), F32),
        compiler_params=_cp(2),
        name="final_norm",
    )(x, g.reshape(1, d))


def _rope_tables(positions, rot_dim, pattern_width, lead_ones):
    inv_freq = ROPE_THETA ** (-jnp.arange(0, rot_dim, 2, dtype=F32) / rot_dim)
    ang = positions.astype(F32)[..., None] * inv_freq
    cos, sin = jnp.cos(ang), jnp.sin(ang)
    shp = cos.shape[:2]
    tail = pattern_width - lead_ones - rot_dim
    c = jnp.concatenate([jnp.ones(shp + (lead_ones,), F32), cos, cos, jnp.ones(shp + (tail,), F32)], -1)
    s = jnp.concatenate([jnp.zeros(shp + (lead_ones,), F32), -sin, sin, jnp.zeros(shp + (tail,), F32)], -1)
    reps = LANES // pattern_width
    return jnp.tile(c, (1, 1, reps)), jnp.tile(s, (1, 1, reps))


def _swap_halves(w, r):
    half = r // 2
    return jnp.concatenate([w[..., half:r], w[..., :half], jnp.zeros_like(w[..., r:])], axis=-1)


def _ab_weights(w_in, w_uq, w_ukv):
    d = w_in.shape[0]
    o = 0
    w_qlat = w_in[:, o:o + A_Q_LORA]; o += A_Q_LORA
    w_kvlat = w_in[:, o:o + A_KV_LORA]; o += A_KV_LORA
    w_kpe = w_in[:, o:o + A_ROPE]; o += A_ROPE
    nb = B_HEADS * 2 * B_DIM
    w_bq = w_in[:, o:o + nb]; o += nb
    w_bk = w_in[:, o:o + nb]; o += nb
    w_bv = w_in[:, o:o + nb]
    zpad = lambda n: jnp.zeros((d, n), w_in.dtype)
    kpe_m = jnp.concatenate([zpad(A_NOPE), w_kpe, zpad(LANES - A_NOPE - A_ROPE)], axis=1)
    kpe_s = jnp.concatenate([zpad(A_NOPE), _swap_halves(w_kpe, A_ROPE), zpad(LANES - A_NOPE - A_ROPE)], axis=1)
    grp = lambda w: w.reshape(d, -1, B_DIM)
    bq_s = _swap_halves(grp(w_bq), B_ROPE).reshape(d, nb)
    bk_s = _swap_halves(grp(w_bk), B_ROPE).reshape(d, nb)
    w1 = jnp.concatenate([w_qlat, w_kvlat, kpe_m, kpe_s, w_bq, bq_s, w_bk, bk_s, w_bv], axis=1).astype(BF16)

    ql = w_uq.shape[0]
    wq3 = w_uq.reshape(ql, A_HEADS, A_NOPE + A_ROPE)
    z = lambda n: jnp.zeros((ql, A_HEADS, n), w_uq.dtype)
    pad = LANES - A_NOPE - A_ROPE
    wq_m = jnp.concatenate([wq3, z(pad)], axis=-1).reshape(ql, A_HEADS * LANES)
    wq_s = jnp.concatenate([z(A_NOPE), _swap_halves(wq3[..., A_NOPE:], A_ROPE), z(pad)], axis=-1)
    wq = jnp.concatenate([wq_m, wq_s.reshape(ql, A_HEADS * LANES)], axis=1).astype(BF16)

    kl = w_ukv.shape[0]
    wkv3 = w_ukv.reshape(kl, A_HEADS, A_NOPE + A_V)
    wk = jnp.concatenate([wkv3[..., :A_NOPE], jnp.zeros((kl, A_HEADS, LANES - A_NOPE), w_ukv.dtype)], axis=-1)
    wv = wkv3[..., A_NOPE:]
    wkv = jnp.concatenate([wk.reshape(kl, A_HEADS * LANES), wv.reshape(kl, A_HEADS * A_V)], axis=1).astype(BF16)
    return w1, wq, wkv


def kernel(x, c, positions, ada_w, ada_b, norm_mix_g, norm_ffn_g, w_in_ab, mla_q_norm_g, mla_w_uq,
           mla_kv_norm_g, mla_w_ukv, diff_lambda, diff_head_norm_g, w_out_ab, w_in_c, rel_bias,
           w_out_c, peer_w_q, peer_sub_keys, peer_u, peer_v, final_norm_g):
    bn, sn, d = x.shape
    depth = ada_w.shape[0]
    mod = _modulation(c, ada_w, ada_b).reshape(depth, bn, 6, 1, d)
    ca, sa = _rope_tables(positions, A_ROPE, LANES, A_NOPE)
    cb, sb = _rope_tables(positions, B_ROPE, B_DIM, 0)
    tq_c = 256

    for l in range(depth):
        sh_m, sc_m, g_m, sh_f, sc_f, g_f = (mod[l, :, t] for t in range(6))
        gm = norm_mix_g[l].reshape(1, d)
        j = l // 2
        if l % 2 == 0:
            lambda_init = 0.8 - 0.6 * math.exp(-0.3 * l)
            w1, wq, wkv = _ab_weights(w_in_ab[j], mla_w_uq[j], mla_w_ukv[j])
            qa, ka, va, qb, kb, vb = _ab_proj(
                x, gm, sc_m, sh_m, w1, mla_q_norm_g[j].reshape(1, -1), wq,
                mla_kv_norm_g[j].reshape(1, -1), wkv, ca, sa, cb, sb)
            a_out = _mla_attention(qa, ka, va)
            b_out = _diff_attention(diff_lambda[j], diff_head_norm_g[j], qb, kb, vb, lambda_init)
            wo = w_out_ab[j].astype(BF16)
            na = A_HEADS * A_V
            x = _out_proj(x, g_m, [a_out, b_out], [wo[:na], wo[na:]])
        else:
            q, k, v = _c_proj(x, gm, sc_m, sh_m, w_in_c[j].astype(BF16))
            pad = ((0, 0), (0, 0), (C_PAST, 0), (0, 0))
            bias = _band_bias(rel_bias[j], tq_c)
            c_out = _chunk_attention(q, jnp.pad(k, pad), jnp.pad(v, pad), bias, tq_c)
            x = _out_proj(x, g_m, [c_out], [w_out_c[j].astype(BF16)])
        x = _peer_ffn(x, norm_ffn_g[l].reshape(1, d), sc_f, sh_f, g_f, peer_w_q[l],
                      peer_sub_keys[l], _pack_table(peer_u[l]), _pack_table(peer_v[l]))
    return _final_norm(x, final_norm_g)
```
